```python
import jax, jax.numpy as jnp
from jax import lax
import numpy as np

D_MODEL = 4096
BATCH = 4
SEQ = 2048
DEPTH = 4
DEC_BATCH = 8
DEC_SEQ = 16
PAST_LEN = 2048

CHUNK = 64
N_A_LAYERS = DEPTH // 2
N_B_LAYERS = DEPTH - N_A_LAYERS
A_CHUNK = 128
A_GROUPS = 8
D_A = D_MODEL
N_HEADS = 32
HEAD_DIM = D_MODEL // N_HEADS
LEFT_CHUNKS = 8
BAND_ROWS = (LEFT_CHUNKS + 1) * CHUNK
WINDOW_ROWS = LEFT_CHUNKS * CHUNK
MAX_REL = 128
N_REL = 2 * MAX_REL + 1
D_FF = ((8 * D_MODEL + 3 * 256 - 1) // (3 * 256)) * 256
EPS = 1e-6
NEG_INF = -1e30

kernel_name = 'yoco_gmlp_chunkband_streaming_encoder'


def rms_norm(x, g):
    xf = x.astype(jnp.float32)
    y = xf * lax.rsqrt(jnp.mean(xf * xf, axis=-1, keepdims=True) + EPS)
    return (y * g.astype(jnp.float32)).astype(x.dtype)


def layer_norm(x, g, b):
    xf = x.astype(jnp.float32)
    xc = xf - jnp.mean(xf, axis=-1, keepdims=True)
    y = xc * lax.rsqrt(jnp.mean(xc * xc, axis=-1, keepdims=True) + EPS)
    return (y * g.astype(jnp.float32) + b.astype(jnp.float32)).astype(x.dtype)


def adaln(c, w, b, n):
    m = jax.nn.silu(c) @ w + b
    return jnp.split(m[:, None, :], n, axis=-1)


def modulated_norm(x, g, shift, scale):
    return rms_norm(x, g) * (1 + scale) + shift


def swiglu(h, w_gu, w_down):
    gate, up = jnp.split(h @ w_gu, 2, axis=-1)
    return (jax.nn.silu(gate) * up) @ w_down


def chunk_mlp(h, w_in, ln_g, ln_b, w_s, b_s, w_out):
    B, T, _ = h.shape
    u, v = jnp.split(jax.nn.gelu(h @ w_in), 2, axis=-1)
    v = layer_norm(v, ln_g, ln_b)
    L = min(T, A_CHUNK)
    n = T // L
    ws = w_s[:, :L, :L] * jnp.tril(jnp.ones((L, L), w_s.dtype))
    vg = v.reshape(B, n, L, A_GROUPS, D_A // A_GROUPS)
    mixed = jnp.einsum('gij,bnjgc->bnigc', ws, vg) + b_s[:, :L].T[None, None, :, :, None]
    y = (u * mixed.reshape(B, T, D_A)) @ w_out
    return y, v


def band_attend(q, k, v, q_pos, k_pos, rel_bias):
    s = jnp.einsum('bqhd,bkhd->bhqk', q, k, preferred_element_type=jnp.float32) * (HEAD_DIM ** -0.5)
    rel = jnp.clip(q_pos[:, None] - k_pos[None, :], -MAX_REL, MAX_REL) + MAX_REL
    s = s + rel_bias[:, rel].astype(jnp.float32)
    qc = q_pos // CHUNK
    kc = k_pos // CHUNK
    ok = (k_pos[None, :] >= 0) & (kc[None, :] <= qc[:, None]) & (kc[None, :] >= qc[:, None] - LEFT_CHUNKS)
    p = jax.nn.softmax(jnp.where(ok, s, NEG_INF), axis=-1)
    return jnp.einsum('bhqk,bkhd->bqhd', p.astype(v.dtype), v)


def attend_prompt(q, k, v, rel_bias):
    B, S, H, Dh = q.shape
    kp = jnp.pad(k, ((0, 0), (WINDOW_ROWS, 0), (0, 0), (0, 0)))
    vp = jnp.pad(v, ((0, 0), (WINDOW_ROWS, 0), (0, 0), (0, 0)))
    offs = jnp.arange(CHUNK)
    band = jnp.arange(BAND_ROWS)

    def one_chunk(ci):
        start = ci * CHUNK
        qc = lax.dynamic_slice_in_dim(q, start, CHUNK, axis=1)
        kb = lax.dynamic_slice_in_dim(kp, start, BAND_ROWS, axis=1)
        vb = lax.dynamic_slice_in_dim(vp, start, BAND_ROWS, axis=1)
        return band_attend(qc, kb, vb, start + offs, start - WINDOW_ROWS + band, rel_bias)

    out = lax.map(one_chunk, jnp.arange(S // CHUNK))
    return jnp.moveaxis(out, 0, 1).reshape(B, S, H, Dh)


def make_attend_sample(cache_k, cache_v):
    R = cache_k.shape[1]

    def attend(q, k, v, rel_bias):
        T = q.shape[1]
        k_all = jnp.concatenate([cache_k, k], axis=1)
        v_all = jnp.concatenate([cache_v, v], axis=1)
        q_pos = PAST_LEN + jnp.arange(T)
        k_pos = jnp.concatenate([PAST_LEN - R + jnp.arange(R), q_pos])
        return band_attend(q, k_all, v_all, q_pos, k_pos, rel_bias)

    return attend


def shared_kv(x, c, kv_w_mod, kv_b_mod, kv_norm_g, w_kv):
    B, T, _ = x.shape
    shift, scale = adaln(c, kv_w_mod, kv_b_mod, 2)
    h = modulated_norm(x, kv_norm_g, shift, scale)
    k, v = jnp.split((h @ w_kv).reshape(B, T, 2 * N_HEADS, HEAD_DIM), 2, axis=2)
    return k, v


def trunk(x, c, attend, w_mod, b_mod, norm_g, a_w_in, a_ln_g, a_ln_b, a_w_s, a_b_s, a_w_out,
          kv_w_mod, kv_b_mod, kv_norm_g, w_kv, b_w_q, b_rel_bias, b_w_o, ffn_w_gu, ffn_w_down):
    B, T, _ = x.shape
    v_rows = []
    k_sh = None
    v_sh = None
    for i in range(DEPTH):
        sh1, sc1, g1, sh2, sc2, g2 = adaln(c, w_mod[i], b_mod[i], 6)
        h = modulated_norm(x, norm_g[i, 0], sh1, sc1)
        if i < N_A_LAYERS:
            y, v_a = chunk_mlp(h, a_w_in[i], a_ln_g[i], a_ln_b[i], a_w_s[i], a_b_s[i], a_w_out[i])
            v_rows.append(v_a)
        else:
            j = i - N_A_LAYERS
            q = (h @ b_w_q[j]).reshape(B, T, N_HEADS, HEAD_DIM)
            y = attend(q, k_sh, v_sh, b_rel_bias[j]).reshape(B, T, N_HEADS * HEAD_DIM) @ b_w_o[j]
        x = x + g1 * rms_norm(y, norm_g[i, 1])
        h = modulated_norm(x, norm_g[i, 2], sh2, sc2)
        x = x + g2 * rms_norm(swiglu(h, ffn_w_gu[i], ffn_w_down[i]), norm_g[i, 3])
        if i == N_A_LAYERS - 1:
            k_sh, v_sh = shared_kv(x, c, kv_w_mod, kv_b_mod, kv_norm_g, w_kv)
    return x, k_sh, v_sh, jnp.stack(v_rows)


def setup_inputs(seed: int = 0) -> dict:
    key = jax.random.key(seed)
    ks = jax.random.split(key, 24)

    def nrm(k, shape, fan_in, gain=1.0):
        return jax.random.normal(k, shape, jnp.float32) * (gain * fan_in ** -0.5)

    def gains(k, shape):
        return 1.0 + 0.05 * jax.random.normal(k, shape, jnp.float32)

    R = min(WINDOW_ROWS, PAST_LEN)
    HD = N_HEADS * HEAD_DIM
    return {
        'x_prompt': jax.random.normal(ks[0], (BATCH, SEQ, D_MODEL), jnp.float32),
        'x_sample': jax.random.normal(ks[1], (DEC_BATCH, DEC_SEQ, D_MODEL), jnp.float32),
        'cache_k': jax.random.normal(ks[2], (DEC_BATCH, R, N_HEADS, HEAD_DIM), jnp.float32),
        'cache_v': jax.random.normal(ks[3], (DEC_BATCH, R, N_HEADS, HEAD_DIM), jnp.float32),
        'c_prompt': jax.random.normal(ks[4], (BATCH, D_MODEL), jnp.float32),
        'c_sample': jax.random.normal(ks[5], (DEC_BATCH, D_MODEL), jnp.float32),
        'w_mod': nrm(ks[6], (DEPTH, D_MODEL, 6 * D_MODEL), D_MODEL, 0.5),
        'b_mod': 0.02 * jax.random.normal(ks[7], (DEPTH, 6 * D_MODEL), jnp.float32),
        'norm_g': gains(ks[8], (DEPTH, 4, D_MODEL)),
        'a_w_in': nrm(ks[9], (N_A_LAYERS, D_MODEL, 2 * D_A), D_MODEL),
        'a_ln_g': gains(ks[10], (N_A_LAYERS, D_A)),
        'a_ln_b': 0.02 * jax.random.normal(ks[11], (N_A_LAYERS, D_A), jnp.float32),
        'a_w_s': nrm(ks[12], (N_A_LAYERS, A_GROUPS, A_CHUNK, A_CHUNK), A_CHUNK),
        'a_b_s': 1.0 + 0.1 * jax.random.normal(ks[13], (N_A_LAYERS, A_GROUPS, A_CHUNK), jnp.float32),
        'a_w_out': nrm(ks[14], (N_A_LAYERS, D_A, D_MODEL), D_A),
        'kv_w_mod': nrm(ks[15], (D_MODEL, 2 * D_MODEL), D_MODEL, 0.5),
        'kv_b_mod': 0.02 * jax.random.normal(ks[16], (2 * D_MODEL,), jnp.float32),
        'kv_norm_g': gains(ks[17], (D_MODEL,)),
        'w_kv': nrm(ks[18], (D_MODEL, 2 * HD), D_MODEL),
        'b_w_q': nrm(ks[19], (N_B_LAYERS, D_MODEL, HD), D_MODEL),
        'b_rel_bias': 0.5 * jax.random.normal(ks[20], (N_B_LAYERS, N_HEADS, N_REL), jnp.float32),
        'b_w_o': nrm(ks[21], (N_B_LAYERS, HD, D_MODEL), HD),
        'ffn_w_gu': nrm(ks[22], (DEPTH, D_MODEL, 2 * D_FF), D_MODEL),
        'ffn_w_down': nrm(ks[23], (DEPTH, D_FF, D_MODEL), D_FF),
    }


def reference(x_prompt, x_sample, cache_k, cache_v, c_prompt, c_sample, w_mod, b_mod, norm_g,
              a_w_in, a_ln_g, a_ln_b, a_w_s, a_b_s, a_w_out, kv_w_mod, kv_b_mod, kv_norm_g, w_kv,
              b_w_q, b_rel_bias, b_w_o, ffn_w_gu, ffn_w_down):
    params = (w_mod, b_mod, norm_g, a_w_in, a_ln_g, a_ln_b, a_w_s, a_b_s, a_w_out,
              kv_w_mod, kv_b_mod, kv_norm_g, w_kv, b_w_q, b_rel_bias, b_w_o, ffn_w_gu, ffn_w_down)
    y_prompt, k_p, v_p, _ = trunk(x_prompt, c_prompt, attend_prompt, *params)
    y_sample, new_k_sample, new_v_sample, new_chunk_v_sample = trunk(
        x_sample, c_sample, make_attend_sample(cache_k, cache_v), *params)
    S = x_prompt.shape[1]
    rows = min(WINDOW_ROWS, S)
    new_k_prompt = k_p[:, S - rows:]
    new_v_prompt = v_p[:, S - rows:]
    return (y_prompt, y_sample, new_k_prompt, new_v_prompt, new_k_sample, new_v_sample, new_chunk_v_sample)
```

```python
import functools

import jax
import jax.numpy as jnp
from jax import lax
from jax.experimental import pallas as pl
from jax.experimental.pallas import tpu as pltpu

F32 = jnp.float32
BF16 = jnp.bfloat16

CHUNK = 64
LEFT_CHUNKS = 8
WINDOW_ROWS = LEFT_CHUNKS * CHUNK
A_CHUNK = 128
HEAD_DIM = 128
MAX_REL = 128
PAST_LEN = 2048
EPS = 1e-6
NEG_INF = -1e30

V7X_VMEM_REQUEST_CAP = 58 * 1024 * 1024
LANES = 128
MXU_COLS = 256

ATTN_QBLOCK = WINDOW_ROWS
ATTN_HEADS_PER_STEP = 2


def _vmem_limit(estimate_bytes):
    return int(min(V7X_VMEM_REQUEST_CAP, estimate_bytes + 16 * 1024 * 1024))


def _params(n_axes, vmem_estimate):
    return pltpu.CompilerParams(
        dimension_semantics=("arbitrary",) * n_axes,
        vmem_limit_bytes=_vmem_limit(vmem_estimate))


def _tile(dim, pref):
    if dim <= pref:
        return dim
    t = (pref // LANES) * LANES
    while t >= LANES:
        if dim % t == 0:
            return t
        t -= LANES
    return dim


def _adaln_kernel(c_ref, w_ref, b_ref, o_ref):
    a = jax.nn.silu(c_ref[...]).astype(BF16)
    w = w_ref[...].astype(BF16)
    o_ref[...] = jnp.dot(a, w, preferred_element_type=F32) + b_ref[...]


def _adaln(c_all, w, b):
    n_layers, d, n = w.shape
    r = c_all.shape[0]
    tn = _tile(n, 512)
    est = 2 * d * tn * 4 + d * tn * 2 + r * d * 4
    return pl.pallas_call(
        _adaln_kernel,
        grid=(n_layers, n // tn),
        in_specs=[
            pl.BlockSpec((r, d), lambda l, j: (0, 0)),
            pl.BlockSpec((None, d, tn), lambda l, j: (l, 0, j)),
            pl.BlockSpec((None, 1, tn), lambda l, j: (l, 0, j)),
        ],
        out_specs=pl.BlockSpec((None, r, tn), lambda l, j: (l, 0, j)),
        out_shape=jax.ShapeDtypeStruct((n_layers, r, n), F32),
        compiler_params=_params(2, est),
        name="adaln",
    )(c_all, w, b.reshape(n_layers, 1, n))


def _rms(x):
    return x * lax.rsqrt(jnp.mean(x * x, axis=-1, keepdims=True) + EPS)


def _rownorm_kernel(*refs, has_resid, n_h):
    pos = 0
    x = refs[pos][...]; pos += 1
    if has_resid:
        y = refs[pos][...]
        gate = refs[pos + 1][...]
        g_post = refs[pos + 2][...]
        pos += 3
        x = x + gate * (_rms(y) * g_post)
    h_params = refs[pos:pos + 3 * n_h]
    outs = refs[pos + 3 * n_h:]
    o = 0
    if has_resid:
        outs[0][...] = x
        o = 1
    if n_h:
        xn = _rms(x)
        for k in range(n_h):
            gain = h_params[3 * k][...]
            shift = h_params[3 * k + 1][...]
            scale = h_params[3 * k + 2][...]
            outs[o + k][...] = ((xn * gain) * (1 + scale) + shift).astype(outs[o + k].dtype)


def _rownorm(x, resid, h_mods, rows_per_step):
    b, t, d = x.shape
    if t >= rows_per_step:
        bb, tr = 1, rows_per_step
    else:
        bb, tr = b, t
    grid = (b // bb, t // tr)
    row_spec = pl.BlockSpec((bb, tr, d), lambda i, j: (i, j, 0))
    seq_spec = pl.BlockSpec((bb, 1, d), lambda i, j: (i, 0, 0))
    par_spec = pl.BlockSpec((1, 1, d), lambda i, j: (0, 0, 0))
    args, in_specs = [x], [row_spec]
    if resid is not None:
        y, gate, g_post = resid
        args += [y, gate, g_post.reshape(1, 1, d)]
        in_specs += [row_spec, seq_spec, par_spec]
    for gain, shift, scale in h_mods:
        args += [gain.reshape(1, 1, d), shift, scale]
        in_specs += [par_spec, seq_spec, seq_spec]
    out_shape, out_specs = [], []
    if resid is not None:
        out_shape.append(jax.ShapeDtypeStruct((b, t, d), F32))
        out_specs.append(row_spec)
    for _ in h_mods:
        out_shape.append(jax.ShapeDtypeStruct((b, t, d), BF16))
        out_specs.append(row_spec)
    blk = bb * tr * d
    est = 2 * blk * (4 * (1 + 2 * (resid is not None)) + 2 * len(h_mods)) + 3 * blk * 4
    return pl.pallas_call(
        functools.partial(_rownorm_kernel, has_resid=resid is not None, n_h=len(h_mods)),
        grid=grid,
        in_specs=in_specs,
        out_specs=out_specs,
        out_shape=out_shape,
        compiler_params=_params(2, est),
        name="rownorm",
    )(*args)


def _mm_rows_kernel(a_ref, as_ref, *rest, n_w, tnw, epilogue, n_out):
    w_refs = rest[:n_w]
    outs_p = rest[n_w:n_w + n_out]
    outs_s = rest[n_w + n_out:n_w + 2 * n_out]
    wb_ref = rest[n_w + 2 * n_out]
    for t in range(n_w):
        wb_ref[:, t * tnw:(t + 1) * tnw] = w_refs[t][...].astype(BF16)

    def compute(a):
        r = jnp.dot(a, wb_ref[...], preferred_element_type=F32)
        if epilogue == "gelu":
            r = jax.nn.gelu(r)
        elif epilogue == "swiglu":
            r = jax.nn.silu(r[:, :tnw]) * r[:, tnw:]
        return r

    r = compute(a_ref[...])
    for o in outs_p:
        o[...] = r.astype(o.dtype)

    @pl.when(pl.program_id(0) == pl.num_programs(0) - 1)
    def _():
        rs = compute(as_ref[...])
        for o in outs_s:
            o[...] = rs.astype(o.dtype)


def _mm_rows(a_p, a_s, w, layer, col_blocks, epilogue, out_dtypes, tm_pref=1024):
    mp, k = a_p.shape
    ms = a_s.shape[0]
    n_w, tnw, n_j, offsets = col_blocks
    tno = tnw if epilogue == "swiglu" else n_w * tnw
    n_out_cols = n_j * tno
    tm = _tile(mp, tm_pref)
    n_i = mp // tm
    in_specs = [
        pl.BlockSpec((tm, k), lambda i, j: (i, 0)),
        pl.BlockSpec((ms, k), lambda i, j: (0, 0)),
    ]
    for t in range(n_w):
        in_specs.append(pl.BlockSpec((None, k, tnw), functools.partial(
            lambda i, j, off: (layer, 0, off + j), off=offsets[t])))
    out_specs, out_shape = [], []
    for dt in out_dtypes:
        out_specs.append(pl.BlockSpec((tm, tno), lambda i, j: (i, j)))
        out_shape.append(jax.ShapeDtypeStruct((mp, n_out_cols), dt))
    for dt in out_dtypes:
        out_specs.append(pl.BlockSpec((ms, tno), lambda i, j: (0, jnp.where(i == n_i - 1, j, 0))))
        out_shape.append(jax.ShapeDtypeStruct((ms, n_out_cols), dt))
    out_bytes = sum(jnp.dtype(dt).itemsize for dt in out_dtypes)
    est = (2 * tm * k * 2 + 2 * ms * k * 2 + 2 * n_w * k * tnw * 4 + n_w * k * tnw * 2
           + 2 * (tm + ms) * tno * out_bytes + tm * n_w * tnw * 4)
    res = pl.pallas_call(
        functools.partial(_mm_rows_kernel, n_w=n_w, tnw=tnw, epilogue=epilogue, n_out=len(out_dtypes)),
        grid=(n_i, n_j),
        in_specs=in_specs,
        out_specs=out_specs,
        out_shape=out_shape,
        scratch_shapes=[pltpu.VMEM((k, n_w * tnw), BF16)],
        compiler_params=_params(2, est),
        name="mm_rows_" + epilogue,
    )(a_p, a_s, *([w] * n_w))
    n = len(out_dtypes)
    return res[:n], res[n:]


def _mm_kacc_kernel(a_ref, as_ref, w_ref, o_ref, os_ref):
    k = pl.program_id(1)
    wb = w_ref[...].astype(BF16)

    @pl.when(k == 0)
    def _():
        o_ref[...] = jnp.zeros_like(o_ref)

    o_ref[...] += jnp.dot(a_ref[...], wb, preferred_element_type=F32)

    @pl.when(pl.program_id(0) == pl.num_programs(0) - 1)
    def _():
        @pl.when(k == 0)
        def _():
            os_ref[...] = jnp.zeros_like(os_ref)

        os_ref[...] += jnp.dot(as_ref[...], wb, preferred_element_type=F32)


def _mm_kacc(a_p, a_s, w, layer, tm_pref=1024, tk_pref=256):
    mp, k = a_p.shape
    ms = a_s.shape[0]
    n = w.shape[2]
    tm = _tile(mp, tm_pref)
    tk = _tile(k, tk_pref)
    est = 2 * tm * tk * 2 + 2 * ms * tk * 2 + 2 * tk * n * 4 + tk * n * 2 + 2 * (tm + ms) * n * 4
    return pl.pallas_call(
        _mm_kacc_kernel,
        grid=(mp // tm, k // tk),
        in_specs=[
            pl.BlockSpec((tm, tk), lambda i, kk: (i, kk)),
            pl.BlockSpec((ms, tk), lambda i, kk: (0, kk)),
            pl.BlockSpec((None, tk, n), lambda i, kk: (layer, kk, 0)),
        ],
        out_specs=[
            pl.BlockSpec((tm, n), lambda i, kk: (i, 0)),
            pl.BlockSpec((ms, n), lambda i, kk: (0, 0)),
        ],
        out_shape=[jax.ShapeDtypeStruct((mp, n), F32), jax.ShapeDtypeStruct((ms, n), F32)],
        compiler_params=_params(2, est),
        name="mm_kacc",
    )(a_p, a_s, w)


def _sgu_kernel(u_ref, v_ref, lng_ref, lnb_ref, ws_ref, msk_ref, bs_ref, o_ref, *maybe_v, n_chunks, n_groups):
    d = v_ref.shape[1]
    gw = d // n_groups
    keep = msk_ref[...] > 0
    for c in range(n_chunks):
        rows = slice(c * A_CHUNK, (c + 1) * A_CHUNK)
        v = v_ref[rows, :].astype(F32)
        vc = v - jnp.mean(v, axis=-1, keepdims=True)
        vn = vc * lax.rsqrt(jnp.mean(vc * vc, axis=-1, keepdims=True) + EPS)
        vn = vn * lng_ref[...] + lnb_ref[...]
        if maybe_v:
            maybe_v[0][rows, :] = vn
        vb = vn.astype(BF16)
        for g in range(n_groups):
            cols = slice(g * gw, (g + 1) * gw)
            wm = jnp.where(keep, ws_ref[g], 0.0).astype(BF16)
            mixed = jnp.dot(wm, vb[:, cols], preferred_element_type=F32) + bs_ref[g]
            o_ref[rows, cols] = (u_ref[rows, cols].astype(F32) * mixed).astype(o_ref.dtype)


def _sgu(g_act, ln_g, ln_b, ws, mask, bs, emit_v, rows_per_step):
    m, two_da = g_act.shape
    da = two_da // 2
    n_groups = ws.shape[0]
    tr = min(rows_per_step, m)
    out_shape = [jax.ShapeDtypeStruct((m, da), BF16)]
    out_specs = [pl.BlockSpec((tr, da), lambda i: (i, 0))]
    if emit_v:
        out_shape.append(jax.ShapeDtypeStruct((m, da), F32))
        out_specs.append(pl.BlockSpec((tr, da), lambda i: (i, 0)))
    est = 2 * tr * da * (2 + 2 + 2 + 4 * emit_v) + 4 * A_CHUNK * da * 4
    return pl.pallas_call(
        functools.partial(_sgu_kernel, n_chunks=tr // A_CHUNK, n_groups=n_groups),
        grid=(m // tr,),
        in_specs=[
            pl.BlockSpec((tr, da), lambda i: (i, 0)),
            pl.BlockSpec((tr, da), lambda i: (i, 1)),
            pl.BlockSpec((1, da), lambda i: (0, 0)),
            pl.BlockSpec((1, da), lambda i: (0, 0)),
            pl.BlockSpec((n_groups, A_CHUNK, A_CHUNK), lambda i: (0, 0, 0)),
            pl.BlockSpec((A_CHUNK, A_CHUNK), lambda i: (0, 0)),
            pl.BlockSpec((n_groups, A_CHUNK, 1), lambda i: (0, 0, 0)),
        ],
        out_specs=out_specs,
        out_shape=out_shape,
        compiler_params=_params(1, est),
        name="sgu",
    )(g_act, g_act, ln_g.reshape(1, da), ln_b.reshape(1, da), ws, mask, bs)


def _softmax_pv(parts):
    m = parts[0][0].max(axis=-1, keepdims=True)
    for s, _ in parts[1:]:
        m = jnp.maximum(m, s.max(axis=-1, keepdims=True))
    l = None
    acc = None
    for s, v in parts:
        p = jnp.exp(s - m)
        ls = p.sum(axis=-1, keepdims=True)
        pv = jnp.dot(p.astype(v.dtype), v, preferred_element_type=F32)
        l = ls if l is None else l + ls
        acc = pv if acc is None else acc + pv
    return acc / l


def _qk(q, k):
    return lax.dot_general(q, k, (((1,), (1,)), ((), ())), preferred_element_type=F32)


def _attn_prompt_kernel(q_ref, kp_ref, kc_ref, vp_ref, vc_ref, b_ref, o_ref, *, heads, scale):
    qb = q_ref.shape[1]
    row = lax.broadcasted_iota(jnp.int32, (qb, 2 * qb), 0)
    col = lax.broadcasted_iota(jnp.int32, (qb, 2 * qb), 1)
    off = col - jnp.bitwise_and(row, -CHUNK)
    first_col = jnp.where(pl.program_id(2) == 0, qb, 0)
    ok = (off >= 0) & (off < WINDOW_ROWS + CHUNK) & (col >= first_col)
    for h in range(heads):
        sl = slice(h * HEAD_DIM, (h + 1) * HEAD_DIM)
        k = jnp.concatenate([kp_ref[0, :, sl], kc_ref[0, :, sl]], axis=0)
        v = jnp.concatenate([vp_ref[0, :, sl], vc_ref[0, :, sl]], axis=0)
        s = _qk(q_ref[0, :, sl], k) * scale + b_ref[h]
        s = jnp.where(ok, s, NEG_INF)
        o_ref[0, :, sl] = _softmax_pv([(s, v)]).astype(o_ref.dtype)


def _attn_prompt(q, kv, bias):
    b, s, d = q.shape
    qb = ATTN_QBLOCK
    hg = ATTN_HEADS_PER_STEP
    w = hg * HEAD_DIM
    n_hg = d // w
    blk = lambda f: pl.BlockSpec((1, qb, w), f)
    est = 2 * 5 * qb * w * 2 + 2 * hg * qb * 2 * qb * 4 + 2 * qb * w * 2 + 8 * qb * 2 * qb * 4
    return pl.pallas_call(
        functools.partial(_attn_prompt_kernel, heads=hg, scale=HEAD_DIM ** -0.5),
        grid=(n_hg, b, s // qb),
        in_specs=[
            blk(lambda g, bi, i: (bi, i, g)),
            blk(lambda g, bi, i: (bi, jnp.maximum(i - 1, 0), g)),
            blk(lambda g, bi, i: (bi, i, g)),
            blk(lambda g, bi, i: (bi, jnp.maximum(i - 1, 0), n_hg + g)),
            blk(lambda g, bi, i: (bi, i, n_hg + g)),
            pl.BlockSpec((hg, qb, 2 * qb), lambda g, bi, i: (g, 0, 0)),
        ],
        out_specs=blk(lambda g, bi, i: (bi, i, g)),
        out_shape=jax.ShapeDtypeStruct((b, s, d), BF16),
        compiler_params=_params(3, est),
        name="attn_prompt",
    )(q, kv, kv, kv, kv, bias)


def _attn_sample_kernel(q_ref, kn_ref, vn_ref, ck_ref, cv_ref, bc_ref, bn_ref, o_ref, *, heads, scale):
    nb = ck_ref.shape[0]
    t = q_ref.shape[0] // nb
    for bi in range(nb):
        rows = slice(bi * t, (bi + 1) * t)
        for h in range(heads):
            sl = slice(h * HEAD_DIM, (h + 1) * HEAD_DIM)
            q = q_ref[rows, sl]
            s_c = _qk(q, ck_ref[bi, :, sl].astype(BF16)) * scale + bc_ref[h]
            s_n = _qk(q, kn_ref[rows, sl]) * scale + bn_ref[h]
            o = _softmax_pv([(s_c, cv_ref[bi, :, sl].astype(BF16)), (s_n, vn_ref[rows, sl])])
            o_ref[rows, sl] = o.astype(o_ref.dtype)


def _attn_sample(q, kv, cache_k, cache_v, bias_c, bias_n):
    ms, d = q.shape
    nb, r, _ = cache_k.shape
    t = ms // nb
    hg = ATTN_HEADS_PER_STEP
    w = hg * HEAD_DIM
    n_hg = d // w
    est = 2 * 3 * ms * w * 2 + 2 * 2 * nb * r * w * 4 + 2 * ms * w * 2
    return pl.pallas_call(
        functools.partial(_attn_sample_kernel, heads=hg, scale=HEAD_DIM ** -0.5),
        grid=(n_hg,),
        in_specs=[
            pl.BlockSpec((ms, w), lambda g: (0, g)),
            pl.BlockSpec((ms, w), lambda g: (0, g)),
            pl.BlockSpec((ms, w), lambda g: (0, n_hg + g)),
            pl.BlockSpec((nb, r, w), lambda g: (0, 0, g)),
            pl.BlockSpec((nb, r, w), lambda g: (0, 0, g)),
            pl.BlockSpec((hg, t, r), lambda g: (g, 0, 0)),
            pl.BlockSpec((hg, t, t), lambda g: (g, 0, 0)),
        ],
        out_specs=pl.BlockSpec((ms, w), lambda g: (0, g)),
        out_shape=jax.ShapeDtypeStruct((ms, d), BF16),
        compiler_params=_params(1, est),
        name="attn_sample",
    )(q, kv, kv, cache_k, cache_v, bias_c, bias_n)


def _rel_table(rel_bias, dist):
    return jnp.take(rel_bias, jnp.clip(dist, -MAX_REL, MAX_REL) + MAX_REL, axis=1)


def _prompt_bias(rel_bias):
    qb = ATTN_QBLOCK
    length = 3 * qb
    t = jnp.arange(length)
    t = jnp.where(t >= length - qb, t - length, t)
    e = _rel_table(rel_bias, qb - t)
    h = e.shape[0]
    diag = jnp.tile(e, (1, qb))[:, :qb * (length - 1)].reshape(h, qb, length - 1)
    return diag[:, :, :2 * qb]


def kernel(x_prompt, x_sample, cache_k, cache_v, c_prompt, c_sample, w_mod, b_mod, norm_g, a_w_in, a_ln_g, a_ln_b, a_w_s, a_b_s, a_w_out, kv_w_mod, kv_b_mod, kv_norm_g, w_kv, b_w_q, b_rel_bias, b_w_o, ffn_w_gu, ffn_w_down):
    bp, sp, d = x_prompt.shape
    bs, ts, _ = x_sample.shape
    depth = w_mod.shape[0]
    n_a = a_w_in.shape[0]
    n_heads = d // HEAD_DIM
    d_ff = ffn_w_down.shape[1]
    d_a = a_w_out.shape[1]
    n_groups = a_w_s.shape[1]
    mp, ms = bp * sp, bs * ts
    r_cache = cache_k.shape[1]
    assert sp % ATTN_QBLOCK == 0 and PAST_LEN % CHUNK == 0 and r_cache <= WINDOW_ROWS and ts <= CHUNK
    assert ms == A_CHUNK and ts <= A_CHUNK and sp % A_CHUNK == 0

    n_c = bp + bs
    c_all = jnp.concatenate([c_prompt, c_sample, jnp.zeros((-n_c % 8, d), F32)], axis=0)
    mods = _adaln(c_all, w_mod, b_mod)
    mods_kv = _adaln(c_all, kv_w_mod[None], kv_b_mod[None])

    def mod(arr, layer, k):
        m = arr[layer, :, k * d:(k + 1) * d]
        return m[:bp].reshape(bp, 1, d), m[bp:n_c].reshape(bs, 1, d)

    rows_p = min(256, sp)
    tn_plain = _tile(d, 2 * MXU_COLS)
    tn_ff = _tile(d_ff, MXU_COLS)

    def plain_cols(n):
        tn = _tile(n, 2 * MXU_COLS)
        return (1, tn, n // tn, (0,))

    tri = jnp.tril(jnp.ones((A_CHUNK, A_CHUNK), F32))
    reps = A_CHUNK // ts
    tri_s = jnp.kron(jnp.eye(reps, dtype=F32), jnp.tril(jnp.ones((ts, ts), F32)))

    xp, xs = x_prompt, x_sample
    sh, sc = mod(mods, 0, 0), mod(mods, 0, 1)
    (hp,) = _rownorm(xp, None, [(norm_g[0, 0], sh[0], sc[0])], rows_p)
    (hs,) = _rownorm(xs, None, [(norm_g[0, 0], sh[1], sc[1])], rows_p)

    kv_p = kv_s = kv_p32 = kv_s32 = None
    v_rows = []
    for l in range(depth):
        hp2, hs2 = hp.reshape(mp, d), hs.reshape(ms, d)
        if l < n_a:
            (g_p,), (g_s,) = _mm_rows(hp2, hs2, a_w_in, l, plain_cols(2 * d_a), "gelu", (BF16,))
            (t_p,) = _sgu(g_p, a_ln_g[l], a_ln_b[l], a_w_s[l], tri, a_b_s[l][:, :, None], False, 256)
            ws_s = jnp.tile(a_w_s[l][:, :ts, :ts], (1, reps, reps))
            b_s = jnp.tile(a_b_s[l][:, :ts], (1, reps))[:, :, None]
            t_s, v_s = _sgu(g_s, a_ln_g[l], a_ln_b[l], ws_s, tri_s, b_s, True, 256)
            v_rows.append(v_s.reshape(bs, ts, d_a))
            (y_p,), (y_s,) = _mm_rows(t_p, t_s, a_w_out, l, plain_cols(d), "none", (F32,))
        else:
            j = l - n_a
            (q_p,), (q_s,) = _mm_rows(hp2, hs2, b_w_q, j, plain_cols(d), "none", (BF16,))
            o_p = _attn_prompt(q_p.reshape(bp, sp, d), kv_p.reshape(bp, sp, 2 * d), _prompt_bias(b_rel_bias[j]))
            tq = jnp.arange(ts)
            bias_c = _rel_table(b_rel_bias[j], (r_cache + tq)[:, None] - jnp.arange(r_cache)[None, :])
            bias_n = _rel_table(b_rel_bias[j], tq[:, None] - tq[None, :])
            o_s = _attn_sample(q_s, kv_s, cache_k.reshape(bs, r_cache, d), cache_v.reshape(bs, r_cache, d),
                               bias_c, bias_n)
            (y_p,), (y_s,) = _mm_rows(o_p.reshape(mp, d), o_s, b_w_o, j, plain_cols(d), "none", (F32,))

        g1, sh2, sc2 = mod(mods, l, 2), mod(mods, l, 3), mod(mods, l, 4)
        xp, hp = _rownorm(xp, (y_p.reshape(bp, sp, d), g1[0], norm_g[l, 1]), [(norm_g[l, 2], sh2[0], sc2[0])], rows_p)
        xs, hs = _rownorm(xs, (y_s.reshape(bs, ts, d), g1[1], norm_g[l, 1]), [(norm_g[l, 2], sh2[1], sc2[1])], rows_p)

        n_ff = d_ff // tn_ff
        (f_p,), (f_s,) = _mm_rows(hp.reshape(mp, d), hs.reshape(ms, d), ffn_w_gu, l,
                                  (2, tn_ff, n_ff, (0, n_ff)), "swiglu", (BF16,))
        y_p, y_s = _mm_kacc(f_p, f_s, ffn_w_down, l)

        g2 = mod(mods, l, 5)
        h_p, h_s = [], []
        if l + 1 < depth:
            sh, sc = mod(mods, l + 1, 0), mod(mods, l + 1, 1)
            h_p.append((norm_g[l + 1, 0], sh[0], sc[0]))
            h_s.append((norm_g[l + 1, 0], sh[1], sc[1]))
        if l == n_a - 1:
            sh, sc = mod(mods_kv, 0, 0), mod(mods_kv, 0, 1)
            h_p.append((kv_norm_g, sh[0], sc[0]))
            h_s.append((kv_norm_g, sh[1], sc[1]))
        res_p = _rownorm(xp, (y_p.reshape(bp, sp, d), g2[0], norm_g[l, 3]), h_p, rows_p)
        res_s = _rownorm(xs, (y_s.reshape(bs, ts, d), g2[1], norm_g[l, 3]), h_s, rows_p)
        xp, xs = res_p[0], res_s[0]
        if l + 1 < depth:
            hp, hs = res_p[1], res_s[1]
        if l == n_a - 1:
            hkv_p, hkv_s = res_p[-1], res_s[-1]
            (kv_p, kv_p32), (kv_s, kv_s32) = _mm_rows(hkv_p.reshape(mp, d), hkv_s.reshape(ms, d), w_kv[None], 0,
                                                     plain_cols(2 * d), "none", (BF16, F32))

    rows = min(WINDOW_ROWS, sp)
    kv_p32 = kv_p32.reshape(bp, sp, 2, n_heads, HEAD_DIM)[:, sp - rows:]
    kv_s32 = kv_s32.reshape(bs, ts, 2, n_heads, HEAD_DIM)
    return (xp, xs, kv_p32[:, :, 0], kv_p32[:, :, 1], kv_s32[:, :, 0], kv_s32[:, :, 1], jnp.stack(v_rows))
```

```python
import functools

import jax
import jax.numpy as jnp
from jax import lax
from jax.experimental import pallas as pl
from jax.experimental.pallas import tpu as pltpu

F32 = jnp.float32
BF16 = jnp.bfloat16

CHUNK = 64
LEFT_CHUNKS = 8
WINDOW_ROWS = LEFT_CHUNKS * CHUNK
BAND_ROWS = WINDOW_ROWS + CHUNK
A_CHUNK = 128
HEAD_DIM = 128
MAX_REL = 128
PAST_LEN = 2048
EPS = 1e-6
NEG_INF = -1e30

V7X_VMEM_REQUEST_CAP = 58 * 1024 * 1024
LANES = 128
MXU_COLS = 256

MM_ROWS = 2048
MM_COLS = 2 * MXU_COLS
KACC_ROWS = 1024
KACC_K = MXU_COLS
NORM_ROWS = 256
SGU_ROWS = 2 * A_CHUNK
ATTN_QBLOCK = 256
ATTN_HEADS_PER_STEP = 4
ATTN_STRIP = 2 * LANES


def _vmem_limit(estimate_bytes):
    return int(min(V7X_VMEM_REQUEST_CAP, estimate_bytes + 16 * 1024 * 1024))


def _params(n_axes, vmem_estimate):
    return pltpu.CompilerParams(
        dimension_semantics=("arbitrary",) * n_axes,
        vmem_limit_bytes=_vmem_limit(vmem_estimate))


def _tile(dim, pref):
    if dim <= pref:
        return dim
    t = (pref // LANES) * LANES
    while t >= LANES:
        if dim % t == 0:
            return t
        t -= LANES
    return dim


def _adaln_kernel(c_ref, w_ref, b_ref, o_ref):
    a = jax.nn.silu(c_ref[...]).astype(BF16)
    w = w_ref[...].astype(BF16)
    o_ref[...] = jnp.dot(a, w, preferred_element_type=F32) + b_ref[...]


def _adaln(c_all, w, b):
    n_layers, d, n = w.shape
    r = c_all.shape[0]
    tn = _tile(n, MM_COLS)
    est = 2 * d * tn * 4 + d * tn * 2 + r * d * 4
    return pl.pallas_call(
        _adaln_kernel,
        grid=(n_layers, n // tn),
        in_specs=[
            pl.BlockSpec((r, d), lambda l, j: (0, 0)),
            pl.BlockSpec((None, d, tn), lambda l, j: (l, 0, j)),
            pl.BlockSpec((None, 1, tn), lambda l, j: (l, 0, j)),
        ],
        out_specs=pl.BlockSpec((None, r, tn), lambda l, j: (l, 0, j)),
        out_shape=jax.ShapeDtypeStruct((n_layers, r, n), F32),
        compiler_params=_params(2, est),
        name="adaln",
    )(c_all, w, b.reshape(n_layers, 1, n))


def _rms(x):
    return x * lax.rsqrt(jnp.mean(x * x, axis=-1, keepdims=True) + EPS)


def _rownorm_kernel(*refs, has_resid, n_h):
    pos = 0
    x = refs[pos][...]; pos += 1
    if has_resid:
        y = refs[pos][...].astype(F32)
        gate = refs[pos + 1][...]
        g_post = refs[pos + 2][...]
        pos += 3
        x = x + gate * (_rms(y) * g_post)
    h_params = refs[pos:pos + 3 * n_h]
    outs = refs[pos + 3 * n_h:]
    o = 0
    if has_resid:
        outs[0][...] = x
        o = 1
    if n_h:
        xn = _rms(x)
        for k in range(n_h):
            gain = h_params[3 * k][...]
            shift = h_params[3 * k + 1][...]
            scale = h_params[3 * k + 2][...]
            outs[o + k][...] = ((xn * gain) * (1 + scale) + shift).astype(outs[o + k].dtype)


def _rownorm(x, resid, h_mods):
    b, t, d = x.shape
    if t >= NORM_ROWS:
        bb, tr = 1, NORM_ROWS
    else:
        bb, tr = b, t
    grid = (b // bb, t // tr)
    row_spec = pl.BlockSpec((bb, tr, d), lambda i, j: (i, j, 0))
    seq_spec = pl.BlockSpec((bb, 1, d), lambda i, j: (i, 0, 0))
    par_spec = pl.BlockSpec((1, 1, d), lambda i, j: (0, 0, 0))
    args, in_specs = [x], [row_spec]
    in_bytes = 4
    if resid is not None:
        y, gate, g_post = resid
        args += [y, gate, g_post.reshape(1, 1, d)]
        in_specs += [row_spec, seq_spec, par_spec]
        in_bytes += y.dtype.itemsize
    for gain, shift, scale in h_mods:
        args += [gain.reshape(1, 1, d), shift, scale]
        in_specs += [par_spec, seq_spec, seq_spec]
    out_shape, out_specs = [], []
    if resid is not None:
        out_shape.append(jax.ShapeDtypeStruct((b, t, d), F32))
        out_specs.append(row_spec)
    for _ in h_mods:
        out_shape.append(jax.ShapeDtypeStruct((b, t, d), BF16))
        out_specs.append(row_spec)
    blk = bb * tr * d
    est = 2 * blk * (in_bytes + 4 * (resid is not None) + 2 * len(h_mods)) + 3 * blk * 4
    return pl.pallas_call(
        functools.partial(_rownorm_kernel, has_resid=resid is not None, n_h=len(h_mods)),
        grid=grid,
        in_specs=in_specs,
        out_specs=out_specs,
        out_shape=out_shape,
        compiler_params=_params(2, est),
        name="rownorm",
    )(*args)


def _mm_rows_kernel(a_ref, as_ref, *rest, n_w, tnw, epilogue, n_out, tail):
    w_refs = rest[:n_w]
    p = n_w
    outs_p = rest[p:p + n_out]; p += n_out
    outs_s = rest[p:p + n_out]; p += n_out
    if tail:
        tail_k, tail_v, samp_k, samp_v = rest[p:p + 4]
        p += 4
    wb_ref = rest[p]
    for t in range(n_w):
        wb_ref[:, t * tnw:(t + 1) * tnw] = w_refs[t][...].astype(BF16)

    def compute(a):
        r = jnp.dot(a, wb_ref[...], preferred_element_type=F32)
        if epilogue == "gelu":
            r = jax.nn.gelu(r)
        elif epilogue == "swiglu":
            r = jax.nn.silu(r[:, :tnw]) * r[:, tnw:]
        return r

    def split_store(val, k_ref, v_ref):
        is_k = pl.program_id(1) < pl.num_programs(1) // 2

        @pl.when(is_k)
        def _():
            k_ref[...] = val

        @pl.when(jnp.logical_not(is_k))
        def _():
            v_ref[...] = val

    r = compute(a_ref[...])
    for o in outs_p:
        o[...] = r.astype(o.dtype)
    if tail:
        split_store(r[r.shape[0] - tail:, :], tail_k, tail_v)

    @pl.when(pl.program_id(0) == pl.num_programs(0) - 1)
    def _():
        rs = compute(as_ref[...])
        for o in outs_s:
            o[...] = rs.astype(o.dtype)
        if tail:
            split_store(rs, samp_k, samp_v)


def _mm_rows(a_p, a_s, w, layer, col_blocks, epilogue, out_dtypes, tm=None, tail=0):
    mp, k = a_p.shape
    ms = a_s.shape[0]
    n_w, tnw, n_j, offsets = col_blocks
    tno = tnw if epilogue == "swiglu" else n_w * tnw
    n_out_cols = n_j * tno
    if tm is None:
        tm = _tile(mp, MM_ROWS)
    n_i = mp // tm
    half = n_j // 2
    last = n_i - 1
    in_specs = [
        pl.BlockSpec((tm, k), lambda i, j: (i, 0), pipeline_mode=pl.Buffered(1)),
        pl.BlockSpec((ms, k), lambda i, j: (0, 0)),
    ]
    for t in range(n_w):
        in_specs.append(pl.BlockSpec((None, k, tnw), functools.partial(
            lambda i, j, off: (layer, 0, off + j), off=offsets[t])))
    out_specs, out_shape = [], []
    for dt in out_dtypes:
        out_specs.append(pl.BlockSpec((tm, tno), lambda i, j: (i, j)))
        out_shape.append(jax.ShapeDtypeStruct((mp, n_out_cols), dt))
    for dt in out_dtypes:
        out_specs.append(pl.BlockSpec((ms, tno), lambda i, j: (0, jnp.where(i == last, j, 0))))
        out_shape.append(jax.ShapeDtypeStruct((ms, n_out_cols), dt))
    if tail:
        assert n_j % 2 == 0 and len(out_dtypes) == 1
        k_col = lambda j: jnp.minimum(j, half - 1)
        v_col = lambda j: jnp.maximum(j - half, 0)
        out_specs += [
            pl.BlockSpec((tail, tno), lambda i, j: (i, k_col(j))),
            pl.BlockSpec((tail, tno), lambda i, j: (i, v_col(j))),
            pl.BlockSpec((ms, tno), lambda i, j: (0, jnp.where(i == last, k_col(j), 0))),
            pl.BlockSpec((ms, tno), lambda i, j: (0, jnp.where(i == last, v_col(j), 0))),
        ]
        out_shape += [jax.ShapeDtypeStruct((n_i * tail, n_out_cols // 2), F32)] * 2
        out_shape += [jax.ShapeDtypeStruct((ms, n_out_cols // 2), F32)] * 2
    out_bytes = sum(jnp.dtype(dt).itemsize for dt in out_dtypes)
    est = (tm * k * 2 + 2 * ms * k * 2 + 2 * n_w * k * tnw * 4 + n_w * k * tnw * 2
           + 2 * (tm + ms) * tno * out_bytes + 4 * (tail + ms) * tno * 4 + tm * n_w * tnw * 4)
    res = pl.pallas_call(
        functools.partial(_mm_rows_kernel, n_w=n_w, tnw=tnw, epilogue=epilogue, n_out=len(out_dtypes), tail=tail),
        grid=(n_i, n_j),
        in_specs=in_specs,
        out_specs=out_specs,
        out_shape=out_shape,
        scratch_shapes=[pltpu.VMEM((k, n_w * tnw), BF16)],
        compiler_params=_params(2, est),
        name="mm_rows_" + epilogue,
    )(a_p, a_s, *([w] * n_w))
    n = len(out_dtypes)
    if tail:
        return res[:n], res[n:2 * n], res[2 * n:]
    return res[:n], res[n:]


def _mm_kacc_kernel(a_ref, as_ref, w_ref, o_ref, os_ref, acc_ref, accs_ref):
    k = pl.program_id(1)
    last_k = pl.num_programs(1) - 1
    wb = w_ref[...].astype(BF16)

    def accumulate(a, acc, out):
        @pl.when(k == 0)
        def _():
            acc[...] = jnp.zeros_like(acc)

        acc[...] += jnp.dot(a[...], wb, preferred_element_type=F32)

        @pl.when(k == last_k)
        def _():
            out[...] = acc[...].astype(out.dtype)

    accumulate(a_ref, acc_ref, o_ref)

    @pl.when(pl.program_id(0) == pl.num_programs(0) - 1)
    def _():
        accumulate(as_ref, accs_ref, os_ref)


def _mm_kacc(a_p, a_s, w, layer):
    mp, k = a_p.shape
    ms = a_s.shape[0]
    n = w.shape[2]
    tm = _tile(mp, KACC_ROWS)
    tk = _tile(k, KACC_K)
    est = (2 * (tm + ms) * tk * 2 + 2 * tk * n * 4 + tk * n * 2 + (tm + ms) * n * 4 + 2 * (tm + ms) * n * 2
           + tm * n * 4)
    return pl.pallas_call(
        _mm_kacc_kernel,
        grid=(mp // tm, k // tk),
        in_specs=[
            pl.BlockSpec((tm, tk), lambda i, kk: (i, kk)),
            pl.BlockSpec((ms, tk), lambda i, kk: (0, kk)),
            pl.BlockSpec((None, tk, n), lambda i, kk: (layer, kk, 0)),
        ],
        out_specs=[
            pl.BlockSpec((tm, n), lambda i, kk: (i, 0)),
            pl.BlockSpec((ms, n), lambda i, kk: (0, 0)),
        ],
        out_shape=[jax.ShapeDtypeStruct((mp, n), BF16), jax.ShapeDtypeStruct((ms, n), BF16)],
        scratch_shapes=[pltpu.VMEM((tm, n), F32), pltpu.VMEM((ms, n), F32)],
        compiler_params=_params(2, est),
        name="mm_kacc",
    )(a_p, a_s, w)


def _sgu_kernel(u_ref, v_ref, lng_ref, lnb_ref, ws_ref, msk_ref, bs_ref, o_ref, *maybe_v, n_chunks, n_groups):
    d = v_ref.shape[1]
    gw = d // n_groups
    keep = msk_ref[...] > 0
    for c in range(n_chunks):
        rows = slice(c * A_CHUNK, (c + 1) * A_CHUNK)
        v = v_ref[rows, :].astype(F32)
        vc = v - jnp.mean(v, axis=-1, keepdims=True)
        vn = vc * lax.rsqrt(jnp.mean(vc * vc, axis=-1, keepdims=True) + EPS)
        vn = vn * lng_ref[...] + lnb_ref[...]
        if maybe_v:
            maybe_v[0][rows, :] = vn
        vb = vn.astype(BF16)
        for g in range(n_groups):
            cols = slice(g * gw, (g + 1) * gw)
            wm = jnp.where(keep, ws_ref[g], 0.0).astype(BF16)
            mixed = jnp.dot(wm, vb[:, cols], preferred_element_type=F32) + bs_ref[g]
            o_ref[rows, cols] = (u_ref[rows, cols].astype(F32) * mixed).astype(o_ref.dtype)


def _sgu(g_act, ln_g, ln_b, ws, mask, bs, emit_v):
    m, two_da = g_act.shape
    da = two_da // 2
    n_groups = ws.shape[0]
    tr = min(SGU_ROWS, m)
    out_shape = [jax.ShapeDtypeStruct((m, da), BF16)]
    out_specs = [pl.BlockSpec((tr, da), lambda i: (i, 0))]
    if emit_v:
        out_shape.append(jax.ShapeDtypeStruct((m, da), F32))
        out_specs.append(pl.BlockSpec((tr, da), lambda i: (i, 0)))
    est = 2 * tr * da * (2 + 2 + 2 + 4 * emit_v) + 4 * A_CHUNK * da * 4
    return pl.pallas_call(
        functools.partial(_sgu_kernel, n_chunks=tr // A_CHUNK, n_groups=n_groups),
        grid=(m // tr,),
        in_specs=[
            pl.BlockSpec((tr, da), lambda i: (i, 0)),
            pl.BlockSpec((tr, da), lambda i: (i, 1)),
            pl.BlockSpec((1, da), lambda i: (0, 0)),
            pl.BlockSpec((1, da), lambda i: (0, 0)),
            pl.BlockSpec((n_groups, A_CHUNK, A_CHUNK), lambda i: (0, 0, 0)),
            pl.BlockSpec((A_CHUNK, A_CHUNK), lambda i: (0, 0)),
            pl.BlockSpec((n_groups, A_CHUNK, 1), lambda i: (0, 0, 0)),
        ],
        out_specs=out_specs,
        out_shape=out_shape,
        compiler_params=_params(1, est),
        name="sgu",
    )(g_act, g_act, ln_g.reshape(1, da), ln_b.reshape(1, da), ws, mask, bs)


def _softmax_pv(parts):
    m = parts[0][0].max(axis=-1, keepdims=True)
    for s, _ in parts[1:]:
        m = jnp.maximum(m, s.max(axis=-1, keepdims=True))
    l = None
    acc = None
    for s, v in parts:
        p = jnp.exp(s - m)
        ls = p.sum(axis=-1, keepdims=True)
        pv = jnp.dot(p.astype(v.dtype), v, preferred_element_type=F32)
        l = ls if l is None else l + ls
        acc = pv if acc is None else acc + pv
    return acc / l


def _qk(q, k):
    return lax.dot_general(q, k, (((1,), (1,)), ((), ())), preferred_element_type=F32)


def _attn_prompt_kernel(q_ref, *rest, heads, nkb, scale):
    k_refs = rest[:nkb]
    v_refs = rest[nkb:2 * nkb]
    strip_ref, o_ref, bias_ref = rest[2 * nkb:]
    qb = q_ref.shape[1]
    kw = nkb * qb

    @pl.when(jnp.logical_and(pl.program_id(1) == 0, pl.program_id(2) == 0))
    def _():
        for h in range(heads):
            far = strip_ref[h, 1, 0:1, 0:1]
            bias_ref[h] = jnp.broadcast_to(far, (qb, kw))
            for c in range(qb // CHUNK):
                odd = c % 2
                start = CHUNK * c + WINDOW_ROWS - ATTN_STRIP // 2 - odd * CHUNK
                bias_ref[h, c * CHUNK:(c + 1) * CHUNK, start:start + ATTN_STRIP] = strip_ref[h, odd]

    row = lax.broadcasted_iota(jnp.int32, (qb, kw), 0)
    col = lax.broadcasted_iota(jnp.int32, (qb, kw), 1)
    off = col - jnp.bitwise_and(row, -CHUNK)
    first_col = jnp.maximum(nkb - 1 - pl.program_id(2), 0) * qb
    ok = (off >= 0) & (off < BAND_ROWS) & (col >= first_col)
    for h in range(heads):
        sl = slice(h * HEAD_DIM, (h + 1) * HEAD_DIM)
        k = jnp.concatenate([r[0, :, sl] for r in k_refs], axis=0)
        v = jnp.concatenate([r[0, :, sl] for r in v_refs], axis=0)
        s = _qk(q_ref[0, :, sl], k) * scale + bias_ref[h]
        s = jnp.where(ok, s, NEG_INF)
        o_ref[0, :, sl] = _softmax_pv([(s, v)]).astype(o_ref.dtype)


def _attn_prompt(q, kv, strips):
    b, s, d = q.shape
    qb = ATTN_QBLOCK
    nkb = WINDOW_ROWS // qb + 1
    hg = ATTN_HEADS_PER_STEP
    w = hg * HEAD_DIM
    n_hg = d // w
    blk = lambda f: pl.BlockSpec((1, qb, w), f)
    k_specs = [blk(functools.partial(lambda g, bi, i, back: (bi, jnp.maximum(i - back, 0), g), back=nkb - 1 - t))
               for t in range(nkb)]
    v_specs = [blk(functools.partial(lambda g, bi, i, back: (bi, jnp.maximum(i - back, 0), n_hg + g), back=nkb - 1 - t))
               for t in range(nkb)]
    est = (2 * (2 + 2 * nkb) * qb * w * 2 + hg * qb * nkb * qb * 4 + 2 * hg * 2 * CHUNK * ATTN_STRIP * 4
           + 8 * qb * nkb * qb * 4)
    return pl.pallas_call(
        functools.partial(_attn_prompt_kernel, heads=hg, nkb=nkb, scale=HEAD_DIM ** -0.5),
        grid=(n_hg, b, s // qb),
        in_specs=[blk(lambda g, bi, i: (bi, i, g))] + k_specs + v_specs + [
            pl.BlockSpec((hg, 2, CHUNK, ATTN_STRIP), lambda g, bi, i: (g, 0, 0, 0))],
        out_specs=blk(lambda g, bi, i: (bi, i, g)),
        out_shape=jax.ShapeDtypeStruct((b, s, d), BF16),
        scratch_shapes=[pltpu.VMEM((hg, qb, nkb * qb), F32)],
        compiler_params=_params(3, est),
        name="attn_prompt",
    )(q, *([kv] * (2 * nkb)), strips)


def _attn_sample_kernel(q_ref, kn_ref, vn_ref, ck_ref, cv_ref, bc_ref, bn_ref, o_ref, *, heads, scale):
    nb = ck_ref.shape[0]
    t = q_ref.shape[0] // nb
    for bi in range(nb):
        rows = slice(bi * t, (bi + 1) * t)
        for h in range(heads):
            sl = slice(h * HEAD_DIM, (h + 1) * HEAD_DIM)
            q = q_ref[rows, sl]
            s_c = _qk(q, ck_ref[bi, :, sl].astype(BF16)) * scale + bc_ref[h]
            s_n = _qk(q, kn_ref[rows, sl]) * scale + bn_ref[h]
            o = _softmax_pv([(s_c, cv_ref[bi, :, sl].astype(BF16)), (s_n, vn_ref[rows, sl])])
            o_ref[rows, sl] = o.astype(o_ref.dtype)


def _attn_sample(q, kv, cache_k, cache_v, bias_c, bias_n):
    ms, d = q.shape
    nb, r, _ = cache_k.shape
    t = ms // nb
    hg = 2
    w = hg * HEAD_DIM
    n_hg = d // w
    est = 2 * 3 * ms * w * 2 + 2 * 2 * nb * r * w * 4 + 2 * ms * w * 2
    return pl.pallas_call(
        functools.partial(_attn_sample_kernel, heads=hg, scale=HEAD_DIM ** -0.5),
        grid=(n_hg,),
        in_specs=[
            pl.BlockSpec((ms, w), lambda g: (0, g)),
            pl.BlockSpec((ms, w), lambda g: (0, g)),
            pl.BlockSpec((ms, w), lambda g: (0, n_hg + g)),
            pl.BlockSpec((nb, r, w), lambda g: (0, 0, g)),
            pl.BlockSpec((nb, r, w), lambda g: (0, 0, g)),
            pl.BlockSpec((hg, t, r), lambda g: (g, 0, 0)),
            pl.BlockSpec((hg, t, t), lambda g: (g, 0, 0)),
        ],
        out_specs=pl.BlockSpec((ms, w), lambda g: (0, g)),
        out_shape=jax.ShapeDtypeStruct((ms, d), BF16),
        compiler_params=_params(1, est),
        name="attn_sample",
    )(q, kv, kv, cache_k, cache_v, bias_c, bias_n)


def _rel_table(rel_bias, dist):
    return jnp.take(rel_bias, jnp.clip(dist, -MAX_REL, MAX_REL) + MAX_REL, axis=1)


def _toeplitz(rel_bias, rows, cols, dist0):
    length = rows + cols
    t = jnp.arange(length)
    t = jnp.where(t >= length - rows, t - length, t)
    e = _rel_table(rel_bias, dist0 - t)
    diag = jnp.tile(e, (1, rows))[:, :rows * (length - 1)].reshape(e.shape[0], rows, length - 1)
    return diag[:, :, :cols]


def _prompt_strips(rel_bias):
    lo = WINDOW_ROWS - ATTN_STRIP // 2 - CHUNK
    wide = _toeplitz(rel_bias, CHUNK, ATTN_STRIP + CHUNK, WINDOW_ROWS - lo)
    return jnp.stack([wide[:, :, CHUNK:], wide[:, :, :ATTN_STRIP]], axis=1)


def kernel(x_prompt, x_sample, cache_k, cache_v, c_prompt, c_sample, w_mod, b_mod, norm_g, a_w_in, a_ln_g, a_ln_b, a_w_s, a_b_s, a_w_out, kv_w_mod, kv_b_mod, kv_norm_g, w_kv, b_w_q, b_rel_bias, b_w_o, ffn_w_gu, ffn_w_down):
    bp, sp, d = x_prompt.shape
    bs, ts, _ = x_sample.shape
    depth = w_mod.shape[0]
    n_a = a_w_in.shape[0]
    n_heads = d // HEAD_DIM
    d_ff = ffn_w_down.shape[1]
    d_a = a_w_out.shape[1]
    mp, ms = bp * sp, bs * ts
    r_cache = cache_k.shape[1]
    tail = min(WINDOW_ROWS, sp)
    assert sp % ATTN_QBLOCK == 0 and WINDOW_ROWS % ATTN_QBLOCK == 0 and ATTN_QBLOCK % (2 * CHUNK) == 0
    assert PAST_LEN % CHUNK == 0 and r_cache == min(WINDOW_ROWS, PAST_LEN) and ts <= CHUNK
    assert ms == A_CHUNK and A_CHUNK % ts == 0 and sp % A_CHUNK == 0

    n_c = bp + bs
    c_all = jnp.concatenate([c_prompt, c_sample, jnp.zeros((-n_c % 8, d), F32)], axis=0)
    mods = _adaln(c_all, w_mod, b_mod)
    mods_kv = _adaln(c_all, kv_w_mod[None], kv_b_mod[None])

    def mod(arr, layer, k):
        m = arr[layer, :, k * d:(k + 1) * d]
        return m[:bp].reshape(bp, 1, d), m[bp:n_c].reshape(bs, 1, d)

    def plain_cols(n):
        tn = _tile(n, MM_COLS)
        return (1, tn, n // tn, (0,))

    tri = jnp.tril(jnp.ones((A_CHUNK, A_CHUNK), F32))
    reps = A_CHUNK // ts
    tri_s = jnp.kron(jnp.eye(reps, dtype=F32), jnp.tril(jnp.ones((ts, ts), F32)))

    xp, xs = x_prompt, x_sample
    sh, sc = mod(mods, 0, 0), mod(mods, 0, 1)
    (hp,) = _rownorm(xp, None, [(norm_g[0, 0], sh[0], sc[0])])
    (hs,) = _rownorm(xs, None, [(norm_g[0, 0], sh[1], sc[1])])

    kv_p = kv_s = tails = None
    v_rows = []
    for l in range(depth):
        hp2, hs2 = hp.reshape(mp, d), hs.reshape(ms, d)
        if l < n_a:
            (g_p,), (g_s,) = _mm_rows(hp2, hs2, a_w_in, l, plain_cols(2 * d_a), "gelu", (BF16,))
            (t_p,) = _sgu(g_p, a_ln_g[l], a_ln_b[l], a_w_s[l], tri, a_b_s[l][:, :, None], False)
            ws_s = jnp.tile(a_w_s[l][:, :ts, :ts], (1, reps, reps))
            b_s = jnp.tile(a_b_s[l][:, :ts], (1, reps))[:, :, None]
            t_s, v_s = _sgu(g_s, a_ln_g[l], a_ln_b[l], ws_s, tri_s, b_s, True)
            v_rows.append(v_s.reshape(bs, ts, d_a))
            (y_p,), (y_s,) = _mm_rows(t_p, t_s, a_w_out, l, plain_cols(d), "none", (BF16,))
        else:
            j = l - n_a
            (q_p,), (q_s,) = _mm_rows(hp2, hs2, b_w_q, j, plain_cols(d), "none", (BF16,))
            o_p = _attn_prompt(q_p.reshape(bp, sp, d), kv_p.reshape(bp, sp, 2 * d), _prompt_strips(b_rel_bias[j]))
            tq = jnp.arange(ts)
            bias_c = _toeplitz(b_rel_bias[j], ts, r_cache, r_cache)
            bias_n = _rel_table(b_rel_bias[j], tq[:, None] - tq[None, :])
            o_s = _attn_sample(q_s, kv_s, cache_k.reshape(bs, r_cache, d), cache_v.reshape(bs, r_cache, d),
                               bias_c, bias_n)
            (y_p,), (y_s,) = _mm_rows(o_p.reshape(mp, d), o_s, b_w_o, j, plain_cols(d), "none", (BF16,))

        g1, sh2, sc2 = mod(mods, l, 2), mod(mods, l, 3), mod(mods, l, 4)
        xp, hp = _rownorm(xp, (y_p.reshape(bp, sp, d), g1[0], norm_g[l, 1]), [(norm_g[l, 2], sh2[0], sc2[0])])
        xs, hs = _rownorm(xs, (y_s.reshape(bs, ts, d), g1[1], norm_g[l, 1]), [(norm_g[l, 2], sh2[1], sc2[1])])

        tn_ff = _tile(d_ff, MM_COLS // 2)
        n_ff = d_ff // tn_ff
        (f_p,), (f_s,) = _mm_rows(hp.reshape(mp, d), hs.reshape(ms, d), ffn_w_gu, l,
                                  (2, tn_ff, n_ff, (0, n_ff)), "swiglu", (BF16,))
        y_p, y_s = _mm_kacc(f_p, f_s, ffn_w_down, l)

        g2 = mod(mods, l, 5)
        h_p, h_s = [], []
        if l + 1 < depth:
            sh, sc = mod(mods, l + 1, 0), mod(mods, l + 1, 1)
            h_p.append((norm_g[l + 1, 0], sh[0], sc[0]))
            h_s.append((norm_g[l + 1, 0], sh[1], sc[1]))
        if l == n_a - 1:
            sh, sc = mod(mods_kv, 0, 0), mod(mods_kv, 0, 1)
            h_p.append((kv_norm_g, sh[0], sc[0]))
            h_s.append((kv_norm_g, sh[1], sc[1]))
        res_p = _rownorm(xp, (y_p.reshape(bp, sp, d), g2[0], norm_g[l, 3]), h_p)
        res_s = _rownorm(xs, (y_s.reshape(bs, ts, d), g2[1], norm_g[l, 3]), h_s)
        xp, xs = res_p[0], res_s[0]
        if l + 1 < depth:
            hp, hs = res_p[1], res_s[1]
        if l == n_a - 1:
            assert sp <= MM_ROWS
            (kv_p,), (kv_s,), tails = _mm_rows(res_p[-1].reshape(mp, d), res_s[-1].reshape(ms, d), w_kv[None], 0,
                                               plain_cols(2 * d), "none", (BF16,), tm=sp, tail=tail)

    k_p, v_p, k_s, v_s = tails
    heads = lambda a, b_, t_: a.reshape(b_, t_, n_heads, HEAD_DIM)
    return (xp, xs, heads(k_p, bp, tail), heads(v_p, bp, tail), heads(k_s, bs, ts), heads(v_s, bs, ts),
            jnp.stack(v_rows))
```

```python
import functools

import jax
import jax.numpy as jnp
from jax import lax
from jax.experimental import pallas as pl
from jax.experimental.pallas import tpu as pltpu

F32 = jnp.float32
BF16 = jnp.bfloat16

CHUNK = 64
LEFT_CHUNKS = 8
WINDOW_ROWS = LEFT_CHUNKS * CHUNK
BAND_ROWS = WINDOW_ROWS + CHUNK
A_CHUNK = 128
HEAD_DIM = 128
MAX_REL = 128
PAST_LEN = 2048
EPS = 1e-6
NEG_INF = -1e30
LOG2E = 1.4426950408889634

V7X_VMEM_REQUEST_CAP = 58 * 1024 * 1024
LANES = 128
MXU_COLS = 256

MM_ROWS = 2048
MM_COLS = 2 * MXU_COLS
MM_ROW_CHUNK = 512
KACC_ROWS = 2048
KACC_COLS = 2048
KACC_K = MXU_COLS
NORM_ROWS = 256
SGU_ROWS = 2 * A_CHUNK
ATTN_QBLOCK = 256
ATTN_HEADS_PER_STEP = 8
ATTN_STRIP = 2 * LANES


def _vmem_limit(estimate_bytes):
    return int(min(V7X_VMEM_REQUEST_CAP, estimate_bytes + 16 * 1024 * 1024))


def _params(n_axes, vmem_estimate):
    return pltpu.CompilerParams(
        dimension_semantics=("arbitrary",) * n_axes,
        vmem_limit_bytes=_vmem_limit(vmem_estimate))


def _tile(dim, pref):
    if dim <= pref:
        return dim
    t = (pref // LANES) * LANES
    while t >= LANES:
        if dim % t == 0:
            return t
        t -= LANES
    return dim


def _adaln_kernel(c_ref, w_ref, b_ref, o_ref):
    a = jax.nn.silu(c_ref[...]).astype(BF16)
    w = w_ref[...].astype(BF16)
    o_ref[...] = jnp.dot(a, w, preferred_element_type=F32) + b_ref[...]


def _adaln(c_all, w, b):
    n_layers, d, n = w.shape
    r = c_all.shape[0]
    tn = _tile(n, MM_COLS)
    est = 2 * d * tn * 4 + d * tn * 2 + r * d * 4
    return pl.pallas_call(
        _adaln_kernel,
        grid=(n_layers, n // tn),
        in_specs=[
            pl.BlockSpec((r, d), lambda l, j: (0, 0)),
            pl.BlockSpec((None, d, tn), lambda l, j: (l, 0, j)),
            pl.BlockSpec((None, 1, tn), lambda l, j: (l, 0, j)),
        ],
        out_specs=pl.BlockSpec((None, r, tn), lambda l, j: (l, 0, j)),
        out_shape=jax.ShapeDtypeStruct((n_layers, r, n), F32),
        compiler_params=_params(2, est),
        name="adaln",
    )(c_all, w, b.reshape(n_layers, 1, n))


def _rms(x):
    return x * lax.rsqrt(jnp.mean(x * x, axis=-1, keepdims=True) + EPS)


def _rownorm_kernel(*refs, has_resid, n_h):
    pos = 0
    x = refs[pos][...]; pos += 1
    if has_resid:
        y = refs[pos][...].astype(F32)
        gate = refs[pos + 1][...]
        g_post = refs[pos + 2][...]
        pos += 3
        x = x + gate * (_rms(y) * g_post)
    h_params = refs[pos:pos + 3 * n_h]
    outs = refs[pos + 3 * n_h:]
    o = 0
    if has_resid:
        outs[0][...] = x
        o = 1
    if n_h:
        xn = _rms(x)
        for k in range(n_h):
            gain = h_params[3 * k][...]
            shift = h_params[3 * k + 1][...]
            scale = h_params[3 * k + 2][...]
            outs[o + k][...] = ((xn * gain) * (1 + scale) + shift).astype(outs[o + k].dtype)


def _rownorm(x, resid, h_mods):
    b, t, d = x.shape
    if t >= NORM_ROWS:
        bb, tr = 1, NORM_ROWS
    else:
        bb, tr = b, t
    grid = (b // bb, t // tr)
    row_spec = pl.BlockSpec((bb, tr, d), lambda i, j: (i, j, 0))
    seq_spec = pl.BlockSpec((bb, 1, d), lambda i, j: (i, 0, 0))
    par_spec = pl.BlockSpec((1, 1, d), lambda i, j: (0, 0, 0))
    args, in_specs = [x], [row_spec]
    in_bytes = 4
    if resid is not None:
        y, gate, g_post = resid
        args += [y, gate, g_post.reshape(1, 1, d)]
        in_specs += [row_spec, seq_spec, par_spec]
        in_bytes += y.dtype.itemsize
    for gain, shift, scale in h_mods:
        args += [gain.reshape(1, 1, d), shift, scale]
        in_specs += [par_spec, seq_spec, seq_spec]
    out_shape, out_specs = [], []
    if resid is not None:
        out_shape.append(jax.ShapeDtypeStruct((b, t, d), F32))
        out_specs.append(row_spec)
    for _ in h_mods:
        out_shape.append(jax.ShapeDtypeStruct((b, t, d), BF16))
        out_specs.append(row_spec)
    blk = bb * tr * d
    est = 2 * blk * (in_bytes + 4 * (resid is not None) + 2 * len(h_mods)) + 3 * blk * 4
    return pl.pallas_call(
        functools.partial(_rownorm_kernel, has_resid=resid is not None, n_h=len(h_mods)),
        grid=grid,
        in_specs=in_specs,
        out_specs=out_specs,
        out_shape=out_shape,
        compiler_params=_params(2, est),
        name="rownorm",
    )(*args)


def _mm_rows_kernel(a_ref, as_ref, *rest, n_w, tnw, epilogue, n_out, tail):
    w_refs = rest[:n_w]
    p = n_w
    outs_p = rest[p:p + n_out]; p += n_out
    outs_s = rest[p:p + n_out]; p += n_out
    if tail:
        tail_k, tail_v, samp_k, samp_v = rest[p:p + 4]
        p += 4
    wb = jnp.concatenate([w[...].astype(BF16) for w in w_refs], axis=1)

    def compute(a):
        r = jnp.dot(a, wb, preferred_element_type=F32)
        if epilogue == "gelu":
            r = jax.nn.gelu(r)
        elif epilogue == "swiglu":
            r = jax.nn.silu(r[:, :tnw]) * r[:, tnw:]
        return r

    def split_store(val, k_ref, v_ref):
        is_k = pl.program_id(1) < pl.num_programs(1) // 2

        @pl.when(is_k)
        def _():
            k_ref[...] = val

        @pl.when(jnp.logical_not(is_k))
        def _():
            v_ref[...] = val

    tm = a_ref.shape[0]
    ch = MM_ROW_CHUNK if tm % MM_ROW_CHUNK == 0 else tm
    r = None
    for c in range(tm // ch):
        rows = slice(c * ch, (c + 1) * ch)
        r = compute(a_ref[rows, :])
        for o in outs_p:
            o[rows, :] = r.astype(o.dtype)
    if tail:
        assert tail == ch
        split_store(r, tail_k, tail_v)

    @pl.when(pl.program_id(0) == pl.num_programs(0) - 1)
    def _():
        rs = compute(as_ref[...])
        for o in outs_s:
            o[...] = rs.astype(o.dtype)
        if tail:
            split_store(rs, samp_k, samp_v)


def _mm_rows(a_p, a_s, w, layer, col_blocks, epilogue, out_dtypes, tm=None, tail=0):
    mp, k = a_p.shape
    ms = a_s.shape[0]
    n_w, tnw, n_j, offsets = col_blocks
    tno = tnw if epilogue == "swiglu" else n_w * tnw
    n_out_cols = n_j * tno
    if tm is None:
        tm = _tile(mp, MM_ROWS)
    n_i = mp // tm
    half = n_j // 2
    last = n_i - 1
    in_specs = [
        pl.BlockSpec((tm, k), lambda i, j: (i, 0), pipeline_mode=pl.Buffered(1)),
        pl.BlockSpec((ms, k), lambda i, j: (0, 0)),
    ]
    for t in range(n_w):
        in_specs.append(pl.BlockSpec((None, k, tnw), functools.partial(
            lambda i, j, off: (layer, 0, off + j), off=offsets[t])))
    out_specs, out_shape = [], []
    for dt in out_dtypes:
        out_specs.append(pl.BlockSpec((tm, tno), lambda i, j: (i, j)))
        out_shape.append(jax.ShapeDtypeStruct((mp, n_out_cols), dt))
    for dt in out_dtypes:
        out_specs.append(pl.BlockSpec((ms, tno), lambda i, j: (0, jnp.where(i == last, j, 0))))
        out_shape.append(jax.ShapeDtypeStruct((ms, n_out_cols), dt))
    if tail:
        assert n_j % 2 == 0 and len(out_dtypes) == 1
        k_col = lambda j: jnp.minimum(j, half - 1)
        v_col = lambda j: jnp.maximum(j - half, 0)
        out_specs += [
            pl.BlockSpec((tail, tno), lambda i, j: (i, k_col(j))),
            pl.BlockSpec((tail, tno), lambda i, j: (i, v_col(j))),
            pl.BlockSpec((ms, tno), lambda i, j: (0, jnp.where(i == last, k_col(j), 0))),
            pl.BlockSpec((ms, tno), lambda i, j: (0, jnp.where(i == last, v_col(j), 0))),
        ]
        out_shape += [jax.ShapeDtypeStruct((n_i * tail, n_out_cols // 2), F32)] * 2
        out_shape += [jax.ShapeDtypeStruct((ms, n_out_cols // 2), F32)] * 2
    out_bytes = sum(jnp.dtype(dt).itemsize for dt in out_dtypes)
    est = (tm * k * 2 + 2 * ms * k * 2 + 2 * n_w * k * tnw * 4 + n_w * k * tnw * 2
           + 2 * (tm + ms) * tno * out_bytes + 4 * (tail + ms) * tno * 4 + tm * n_w * tnw * 4)
    res = pl.pallas_call(
        functools.partial(_mm_rows_kernel, n_w=n_w, tnw=tnw, epilogue=epilogue, n_out=len(out_dtypes), tail=tail),
        grid=(n_i, n_j),
        in_specs=in_specs,
        out_specs=out_specs,
        out_shape=out_shape,
        compiler_params=_params(2, est),
        name="mm_rows_" + epilogue,
    )(a_p, a_s, *([w] * n_w))
    n = len(out_dtypes)
    if tail:
        return res[:n], res[n:2 * n], res[2 * n:]
    return res[:n], res[n:]


def _mm_kacc_kernel(a0_ref, a1_ref, as0_ref, as1_ref, w0_ref, w1_ref, o_ref, os_ref, acc_ref, accs_ref, *, odd_tiles):
    kk = pl.program_id(2)
    last_kk = pl.num_programs(2) - 1

    def zero_first(acc):
        @pl.when(kk == 0)
        def _():
            acc[...] = jnp.zeros_like(acc)

    def accumulate(a0, a1, acc, out):
        a = jnp.concatenate([a0[...], a1[...]], axis=1)
        acc[...] += jnp.dot(a, wb, preferred_element_type=F32)

        @pl.when(kk == last_kk)
        def _():
            out[...] = acc[...].astype(out.dtype)

    zero_first(acc_ref)
    w1 = w1_ref[...]
    if odd_tiles:
        w1 = jnp.where(kk < last_kk, w1, 0.0)
    wb = jnp.concatenate([w0_ref[...].astype(BF16), w1.astype(BF16)], axis=0)
    accumulate(a0_ref, a1_ref, acc_ref, o_ref)

    @pl.when(pl.program_id(0) == pl.num_programs(0) - 1)
    def _():
        zero_first(accs_ref)
        accumulate(as0_ref, as1_ref, accs_ref, os_ref)


def _mm_kacc(a_p, a_s, w, layer):
    mp, k = a_p.shape
    ms = a_s.shape[0]
    n = w.shape[2]
    tm = _tile(mp, KACC_ROWS)
    tn = _tile(n, KACC_COLS)
    tk = _tile(k, KACC_K)
    n_k = k // tk
    n_kk = (n_k + 1) // 2
    n_i = mp // tm
    first = lambda kk: 2 * kk
    second = lambda kk: jnp.minimum(2 * kk + 1, n_k - 1)
    est = (4 * 2 * (tm + ms) * tk * 2 + 2 * 2 * tk * tn * 4 + 2 * tk * tn * 2 + (tm + ms) * tn * 4
           + 2 * (tm + ms) * tn * 2 + tm * tn * 4)
    return pl.pallas_call(
        functools.partial(_mm_kacc_kernel, odd_tiles=n_k % 2 == 1),
        grid=(n_i, n // tn, n_kk),
        in_specs=[
            pl.BlockSpec((tm, tk), lambda i, j, kk: (i, first(kk))),
            pl.BlockSpec((tm, tk), lambda i, j, kk: (i, second(kk))),
            pl.BlockSpec((ms, tk), lambda i, j, kk: (0, first(kk))),
            pl.BlockSpec((ms, tk), lambda i, j, kk: (0, second(kk))),
            pl.BlockSpec((None, tk, tn), lambda i, j, kk: (layer, first(kk), j)),
            pl.BlockSpec((None, tk, tn), lambda i, j, kk: (layer, second(kk), j)),
        ],
        out_specs=[
            pl.BlockSpec((tm, tn), lambda i, j, kk: (i, j)),
            pl.BlockSpec((ms, tn), lambda i, j, kk: (0, jnp.where(i == n_i - 1, j, 0))),
        ],
        out_shape=[jax.ShapeDtypeStruct((mp, n), BF16), jax.ShapeDtypeStruct((ms, n), BF16)],
        scratch_shapes=[pltpu.VMEM((tm, tn), F32), pltpu.VMEM((ms, tn), F32)],
        compiler_params=_params(3, est),
        name="mm_kacc",
    )(a_p, a_p, a_s, a_s, w, w)


def _sgu_kernel(u_ref, v_ref, lng_ref, lnb_ref, ws_ref, msk_ref, bs_ref, o_ref, *maybe_v, n_chunks, n_groups):
    d = v_ref.shape[1]
    gw = d // n_groups
    keep = msk_ref[...] > 0
    for c in range(n_chunks):
        rows = slice(c * A_CHUNK, (c + 1) * A_CHUNK)
        v = v_ref[rows, :].astype(F32)
        vc = v - jnp.mean(v, axis=-1, keepdims=True)
        vn = vc * lax.rsqrt(jnp.mean(vc * vc, axis=-1, keepdims=True) + EPS)
        vn = vn * lng_ref[...] + lnb_ref[...]
        if maybe_v:
            maybe_v[0][rows, :] = vn
        vb = vn.astype(BF16)
        for g in range(n_groups):
            cols = slice(g * gw, (g + 1) * gw)
            wm = jnp.where(keep, ws_ref[g], 0.0).astype(BF16)
            mixed = jnp.dot(wm, vb[:, cols], preferred_element_type=F32) + bs_ref[g]
            o_ref[rows, cols] = (u_ref[rows, cols].astype(F32) * mixed).astype(o_ref.dtype)


def _sgu(g_act, ln_g, ln_b, ws, mask, bs, emit_v):
    m, two_da = g_act.shape
    da = two_da // 2
    n_groups = ws.shape[0]
    tr = min(SGU_ROWS, m)
    out_shape = [jax.ShapeDtypeStruct((m, da), BF16)]
    out_specs = [pl.BlockSpec((tr, da), lambda i: (i, 0))]
    if emit_v:
        out_shape.append(jax.ShapeDtypeStruct((m, da), F32))
        out_specs.append(pl.BlockSpec((tr, da), lambda i: (i, 0)))
    est = 2 * tr * da * (2 + 2 + 2 + 4 * emit_v) + 4 * A_CHUNK * da * 4
    return pl.pallas_call(
        functools.partial(_sgu_kernel, n_chunks=tr // A_CHUNK, n_groups=n_groups),
        grid=(m // tr,),
        in_specs=[
            pl.BlockSpec((tr, da), lambda i: (i, 0)),
            pl.BlockSpec((tr, da), lambda i: (i, 1)),
            pl.BlockSpec((1, da), lambda i: (0, 0)),
            pl.BlockSpec((1, da), lambda i: (0, 0)),
            pl.BlockSpec((n_groups, A_CHUNK, A_CHUNK), lambda i: (0, 0, 0)),
            pl.BlockSpec((A_CHUNK, A_CHUNK), lambda i: (0, 0)),
            pl.BlockSpec((n_groups, A_CHUNK, 1), lambda i: (0, 0, 0)),
        ],
        out_specs=out_specs,
        out_shape=out_shape,
        compiler_params=_params(1, est),
        name="sgu",
    )(g_act, g_act, ln_g.reshape(1, da), ln_b.reshape(1, da), ws, mask, bs)


def _softmax_pv(parts):
    m = parts[0][0].max(axis=-1, keepdims=True)
    for s, _ in parts[1:]:
        m = jnp.maximum(m, s.max(axis=-1, keepdims=True))
    l = None
    acc = None
    for s, v in parts:
        p = jnp.exp2(s - m)
        ls = p.sum(axis=-1, keepdims=True)
        pv = jnp.dot(p.astype(v.dtype), v, preferred_element_type=F32)
        l = ls if l is None else l + ls
        acc = pv if acc is None else acc + pv
    return acc / l


def _qk(q, k):
    return lax.dot_general(q, k, (((1,), (1,)), ((), ())), preferred_element_type=F32)


def _attn_prompt_kernel(q_ref, *rest, heads, nkb, scale):
    k_refs = rest[:nkb]
    v_refs = rest[nkb:2 * nkb]
    strip_ref, o_ref, bias_ref = rest[2 * nkb:]
    qb = q_ref.shape[1]
    kw = nkb * qb
    col = lax.broadcasted_iota(jnp.int32, (qb, kw), 1)

    @pl.when(jnp.logical_and(pl.program_id(1) == 0, pl.program_id(2) == 0))
    def _():
        row = lax.broadcasted_iota(jnp.int32, (qb, kw), 0)
        off = col - jnp.bitwise_and(row, -CHUNK)
        in_band = (off >= 0) & (off < BAND_ROWS)
        for h in range(heads):
            far = strip_ref[h, 1, 0:1, 0:1] * LOG2E
            bias_ref[h] = jnp.broadcast_to(far, (qb, kw))
            for c in range(qb // CHUNK):
                odd = c % 2
                start = CHUNK * c + WINDOW_ROWS - ATTN_STRIP // 2 - odd * CHUNK
                bias_ref[h, c * CHUNK:(c + 1) * CHUNK, start:start + ATTN_STRIP] = strip_ref[h, odd] * LOG2E
            bias_ref[h] = jnp.where(in_band, bias_ref[h], NEG_INF)

    def attend(first_col):
        for h in range(heads):
            sl = slice(h * HEAD_DIM, (h + 1) * HEAD_DIM)
            k = jnp.concatenate([r[0, :, sl] for r in k_refs], axis=0)
            v = jnp.concatenate([r[0, :, sl] for r in v_refs], axis=0)
            s = _qk(q_ref[0, :, sl], k) * (scale * LOG2E) + bias_ref[h]
            if first_col is not None:
                s = jnp.where(col >= first_col, s, NEG_INF)
            o_ref[0, :, sl] = _softmax_pv([(s, v)]).astype(o_ref.dtype)

    blocks_missing = nkb - 1 - pl.program_id(2)

    @pl.when(blocks_missing <= 0)
    def _():
        attend(None)

    @pl.when(blocks_missing > 0)
    def _():
        attend(blocks_missing * qb)


def _attn_prompt(q, kv, strips):
    b, s, d = q.shape
    qb = ATTN_QBLOCK
    nkb = WINDOW_ROWS // qb + 1
    hg = ATTN_HEADS_PER_STEP
    w = hg * HEAD_DIM
    n_hg = d // w
    blk = lambda f: pl.BlockSpec((1, qb, w), f)
    k_specs = [blk(functools.partial(lambda g, bi, i, back: (bi, jnp.maximum(i - back, 0), g), back=nkb - 1 - t))
               for t in range(nkb)]
    v_specs = [blk(functools.partial(lambda g, bi, i, back: (bi, jnp.maximum(i - back, 0), n_hg + g), back=nkb - 1 - t))
               for t in range(nkb)]
    est = (2 * (2 + 2 * nkb) * qb * w * 2 + hg * qb * nkb * qb * 4 + 2 * hg * 2 * CHUNK * ATTN_STRIP * 4
           + 8 * qb * nkb * qb * 4)
    return pl.pallas_call(
        functools.partial(_attn_prompt_kernel, heads=hg, nkb=nkb, scale=HEAD_DIM ** -0.5),
        grid=(n_hg, b, s // qb),
        in_specs=[blk(lambda g, bi, i: (bi, i, g))] + k_specs + v_specs + [
            pl.BlockSpec((hg, 2, CHUNK, ATTN_STRIP), lambda g, bi, i: (g, 0, 0, 0))],
        out_specs=blk(lambda g, bi, i: (bi, i, g)),
        out_shape=jax.ShapeDtypeStruct((b, s, d), BF16),
        scratch_shapes=[pltpu.VMEM((hg, qb, nkb * qb), F32)],
        compiler_params=_params(3, est),
        name="attn_prompt",
    )(q, *([kv] * (2 * nkb)), strips)


def _attn_sample_kernel(q_ref, kn_ref, vn_ref, ck_ref, cv_ref, bc_ref, bn_ref, o_ref, *, heads, scale):
    nb = ck_ref.shape[0]
    t = q_ref.shape[0] // nb
    for bi in range(nb):
        rows = slice(bi * t, (bi + 1) * t)
        for h in range(heads):
            sl = slice(h * HEAD_DIM, (h + 1) * HEAD_DIM)
            q = q_ref[rows, sl]
            s_c = (_qk(q, ck_ref[bi, :, sl].astype(BF16)) * scale + bc_ref[h]) * LOG2E
            s_n = (_qk(q, kn_ref[rows, sl]) * scale + bn_ref[h]) * LOG2E
            o = _softmax_pv([(s_c, cv_ref[bi, :, sl].astype(BF16)), (s_n, vn_ref[rows, sl])])
            o_ref[rows, sl] = o.astype(o_ref.dtype)


def _attn_sample(q, kv, cache_k, cache_v, bias_c, bias_n):
    ms, d = q.shape
    nb, r, _ = cache_k.shape
    t = ms // nb
    hg = 2
    w = hg * HEAD_DIM
    n_hg = d // w
    est = 2 * 3 * ms * w * 2 + 2 * 2 * nb * r * w * 4 + 2 * ms * w * 2
    return pl.pallas_call(
        functools.partial(_attn_sample_kernel, heads=hg, scale=HEAD_DIM ** -0.5),
        grid=(n_hg,),
        in_specs=[
            pl.BlockSpec((ms, w), lambda g: (0, g)),
            pl.BlockSpec((ms, w), lambda g: (0, g)),
            pl.BlockSpec((ms, w), lambda g: (0, n_hg + g)),
            pl.BlockSpec((nb, r, w), lambda g: (0, 0, g)),
            pl.BlockSpec((nb, r, w), lambda g: (0, 0, g)),
            pl.BlockSpec((hg, t, r), lambda g: (g, 0, 0)),
            pl.BlockSpec((hg, t, t), lambda g: (g, 0, 0)),
        ],
        out_specs=pl.BlockSpec((ms, w), lambda g: (0, g)),
        out_shape=jax.ShapeDtypeStruct((ms, d), BF16),
        compiler_params=_params(1, est),
        name="attn_sample",
    )(q, kv, kv, cache_k, cache_v, bias_c, bias_n)


def _rel_table(rel_bias, dist):
    return jnp.take(rel_bias, jnp.clip(dist, -MAX_REL, MAX_REL) + MAX_REL, axis=1)


def _toeplitz(rel_bias, rows, cols, dist0):
    t = jnp.arange(rows + cols - 1) - (rows - 1)
    e = _rel_table(rel_bias, dist0 - t)
    return jnp.stack([e[:, rows - 1 - r:rows - 1 - r + cols] for r in range(rows)], axis=1)


def _prompt_strips(rel_bias):
    lo = WINDOW_ROWS - ATTN_STRIP // 2 - CHUNK
    wide = _toeplitz(rel_bias, CHUNK, ATTN_STRIP + CHUNK, WINDOW_ROWS - lo)
    return jnp.stack([wide[:, :, CHUNK:], wide[:, :, :ATTN_STRIP]], axis=1)


def kernel(x_prompt, x_sample, cache_k, cache_v, c_prompt, c_sample, w_mod, b_mod, norm_g, a_w_in, a_ln_g, a_ln_b, a_w_s, a_b_s, a_w_out, kv_w_mod, kv_b_mod, kv_norm_g, w_kv, b_w_q, b_rel_bias, b_w_o, ffn_w_gu, ffn_w_down):
    bp, sp, d = x_prompt.shape
    bs, ts, _ = x_sample.shape
    depth = w_mod.shape[0]
    n_a = a_w_in.shape[0]
    n_heads = d // HEAD_DIM
    d_ff = ffn_w_down.shape[1]
    d_a = a_w_out.shape[1]
    mp, ms = bp * sp, bs * ts
    r_cache = cache_k.shape[1]
    tail = min(WINDOW_ROWS, sp)
    assert sp % ATTN_QBLOCK == 0 and WINDOW_ROWS % ATTN_QBLOCK == 0 and ATTN_QBLOCK % (2 * CHUNK) == 0
    assert PAST_LEN % CHUNK == 0 and r_cache == min(WINDOW_ROWS, PAST_LEN) and ts <= CHUNK
    assert ms == A_CHUNK and A_CHUNK % ts == 0 and sp % A_CHUNK == 0

    n_c = bp + bs
    c_all = jnp.concatenate([c_prompt, c_sample, jnp.zeros((-n_c % 8, d), F32)], axis=0)
    mods = _adaln(c_all, w_mod, b_mod)
    mods_kv = _adaln(c_all, kv_w_mod[None], kv_b_mod[None])

    def mod(arr, layer, k):
        m = arr[layer, :, k * d:(k + 1) * d]
        return m[:bp].reshape(bp, 1, d), m[bp:n_c].reshape(bs, 1, d)

    def plain_cols(n):
        tn = _tile(n, MM_COLS)
        return (1, tn, n // tn, (0,))

    tri = jnp.tril(jnp.ones((A_CHUNK, A_CHUNK), F32))
    reps = A_CHUNK // ts
    tri_s = jnp.kron(jnp.eye(reps, dtype=F32), jnp.tril(jnp.ones((ts, ts), F32)))

    xp, xs = x_prompt, x_sample
    sh, sc = mod(mods, 0, 0), mod(mods, 0, 1)
    (hp,) = _rownorm(xp, None, [(norm_g[0, 0], sh[0], sc[0])])
    (hs,) = _rownorm(xs, None, [(norm_g[0, 0], sh[1], sc[1])])

    kv_p = kv_s = tails = None
    v_rows = []
    for l in range(depth):
        hp2, hs2 = hp.reshape(mp, d), hs.reshape(ms, d)
        if l < n_a:
            (g_p,), (g_s,) = _mm_rows(hp2, hs2, a_w_in, l, plain_cols(2 * d_a), "gelu", (BF16,))
            (t_p,) = _sgu(g_p, a_ln_g[l], a_ln_b[l], a_w_s[l], tri, a_b_s[l][:, :, None], False)
            ws_s = jnp.tile(a_w_s[l][:, :ts, :ts], (1, reps, reps))
            b_s = jnp.tile(a_b_s[l][:, :ts], (1, reps))[:, :, None]
            t_s, v_s = _sgu(g_s, a_ln_g[l], a_ln_b[l], ws_s, tri_s, b_s, True)
            v_rows.append(v_s.reshape(bs, ts, d_a))
            (y_p,), (y_s,) = _mm_rows(t_p, t_s, a_w_out, l, plain_cols(d), "none", (BF16,))
        else:
            j = l - n_a
            (q_p,), (q_s,) = _mm_rows(hp2, hs2, b_w_q, j, plain_cols(d), "none", (BF16,))
            o_p = _attn_prompt(q_p.reshape(bp, sp, d), kv_p.reshape(bp, sp, 2 * d), _prompt_strips(b_rel_bias[j]))
            tq = jnp.arange(ts)
            bias_c = _toeplitz(b_rel_bias[j], ts, r_cache, r_cache)
            bias_n = _rel_table(b_rel_bias[j], tq[:, None] - tq[None, :])
            o_s = _attn_sample(q_s, kv_s, cache_k.reshape(bs, r_cache, d), cache_v.reshape(bs, r_cache, d),
                               bias_c, bias_n)
            (y_p,), (y_s,) = _mm_rows(o_p.reshape(mp, d), o_s, b_w_o, j, plain_cols(d), "none", (BF16,))

        g1, sh2, sc2 = mod(mods, l, 2), mod(mods, l, 3), mod(mods, l, 4)
        xp, hp = _rownorm(xp, (y_p.reshape(bp, sp, d), g1[0], norm_g[l, 1]), [(norm_g[l, 2], sh2[0], sc2[0])])
        xs, hs = _rownorm(xs, (y_s.reshape(bs, ts, d), g1[1], norm_g[l, 1]), [(norm_g[l, 2], sh2[1], sc2[1])])

        tn_ff = _tile(d_ff, MM_COLS // 2)
        n_ff = d_ff // tn_ff
        (f_p,), (f_s,) = _mm_rows(hp.reshape(mp, d), hs.reshape(ms, d), ffn_w_gu, l,
                                  (2, tn_ff, n_ff, (0, n_ff)), "swiglu", (BF16,))
        y_p, y_s = _mm_kacc(f_p, f_s, ffn_w_down, l)

        g2 = mod(mods, l, 5)
        h_p, h_s = [], []
        if l + 1 < depth:
            sh, sc = mod(mods, l + 1, 0), mod(mods, l + 1, 1)
            h_p.append((norm_g[l + 1, 0], sh[0], sc[0]))
            h_s.append((norm_g[l + 1, 0], sh[1], sc[1]))
        if l == n_a - 1:
            sh, sc = mod(mods_kv, 0, 0), mod(mods_kv, 0, 1)
            h_p.append((kv_norm_g, sh[0], sc[0]))
            h_s.append((kv_norm_g, sh[1], sc[1]))
        res_p = _rownorm(xp, (y_p.reshape(bp, sp, d), g2[0], norm_g[l, 3]), h_p)
        res_s = _rownorm(xs, (y_s.reshape(bs, ts, d), g2[1], norm_g[l, 3]), h_s)
        xp, xs = res_p[0], res_s[0]
        if l + 1 < depth:
            hp, hs = res_p[1], res_s[1]
        if l == n_a - 1:
            assert sp <= MM_ROWS
            (kv_p,), (kv_s,), tails = _mm_rows(res_p[-1].reshape(mp, d), res_s[-1].reshape(ms, d), w_kv[None], 0,
                                               plain_cols(2 * d), "none", (BF16,), tm=sp, tail=tail)

    k_p, v_p, k_s, v_s = tails
    heads = lambda a, b_, t_: a.reshape(b_, t_, n_heads, HEAD_DIM)
    return (xp, xs, heads(k_p, bp, tail), heads(v_p, bp, tail), heads(k_s, bs, ts), heads(v_s, bs, ts),
            jnp.stack(v_rows))
```
